```python
import math
import jax, jax.numpy as jnp
from jax import lax
import numpy as np

D_MODEL = 1024
BATCH = 16
SEQ = 2048
DEPTH = 1
DEC_BATCH = 32
DEC_SEQ = 32
PAST_LEN = 2048

CHUNK = 64
N_META = 16
ATT_HEADS = 8
HEAD_DIM = 64
ATT_WIDTH = ATT_HEADS * HEAD_DIM
CONV_CH = D_MODEL - ATT_WIDTH
CONV_K = 31
D_FF = 4 * D_MODEL
Q_BLOCK = 128
IN_COLS = 3 * ATT_WIDTH + 2 * CONV_CH
ALPHA = (2.0 * DEPTH) ** 0.25
BETA_INIT = (8.0 * DEPTH) ** -0.25
LN_EPS = 1e-5

kernel_name = "hymba_stickbreak_conformer_stream_step"


def layer_norm(x, g, b):
    xf = x.astype(jnp.float32)
    mu = jnp.mean(xf, axis=-1, keepdims=True)
    var = jnp.mean(jnp.square(xf - mu), axis=-1, keepdims=True)
    y = (xf - mu) * lax.rsqrt(var + LN_EPS) * g.astype(jnp.float32) + b.astype(jnp.float32)
    return y.astype(x.dtype)


def project_in(h, w):
    B, T, _ = h.shape
    p = h @ w
    q, k, v, a, gate = jnp.split(
        p, [ATT_WIDTH, 2 * ATT_WIDTH, 3 * ATT_WIDTH, 3 * ATT_WIDTH + CONV_CH], axis=-1)
    shp = (B, T, ATT_HEADS, HEAD_DIM)
    return q.reshape(shp), k.reshape(shp), v.reshape(shp), a, gate


def stick_breaking_block(q, k, v, q_pos, k_pos):
    z = jnp.einsum('bqhd,bshd->bhqs', q, k).astype(jnp.float32) / math.sqrt(HEAD_DIM)
    mask = k_pos[None, :] < q_pos[:, None]
    log_beta = jax.nn.log_sigmoid(z)
    log_1m = jnp.where(mask, jax.nn.log_sigmoid(-z), 0.0)
    suffix = lax.cumsum(log_1m, axis=3, reverse=True) - log_1m
    a = jnp.where(mask, jnp.exp(log_beta + suffix), 0.0)
    return jnp.einsum('bhqs,bshd->bqhd', a.astype(v.dtype), v)


def stick_breaking_prompt(q, k, v):
    B, L, H, dh = q.shape
    nb = -(-L // Q_BLOCK)
    lp = nb * Q_BLOCK
    qp = jnp.pad(q, ((0, 0), (0, lp - L), (0, 0), (0, 0)))
    q_blocks = qp.reshape(B, nb, Q_BLOCK, H, dh).transpose(1, 0, 2, 3, 4)
    pos_blocks = jnp.arange(lp, dtype=jnp.int32).reshape(nb, Q_BLOCK)
    k_pos = jnp.arange(L, dtype=jnp.int32)
    out = lax.map(lambda qb: stick_breaking_block(qb[0], k, v, qb[1], k_pos),
                  (q_blocks, pos_blocks))
    return out.transpose(1, 0, 2, 3, 4).reshape(B, lp, H, dh)[:, :L]


def conv_module(a, gate, buf, w_dw, b_dw, g_c, b_c):
    u = a * jax.nn.sigmoid(gate)
    xp = jnp.concatenate([buf, u], axis=1)
    c = lax.conv_general_dilated(
        xp, w_dw[:, None, :], window_strides=(1,), padding='VALID',
        dimension_numbers=('NWC', 'WIO', 'NWC'), feature_group_count=CONV_CH)
    c = jax.nn.silu(layer_norm(c + b_dw, g_c, b_c))
    return c, xp[:, -(CONV_K - 1):]


def layer_tail(h, att, conv, w_out, g1, b1, w_ff1, w_ff2, g2, b2):
    B, T, _ = h.shape
    mix = jnp.concatenate([att.reshape(B, T, ATT_WIDTH), conv], axis=-1) @ w_out
    h1 = layer_norm(ALPHA * h + mix, g1, b1)
    f = jnp.square(jax.nn.relu(h1 @ w_ff1)) @ w_ff2
    return layer_norm(ALPHA * h1 + f, g2, b2)


def setup_inputs(seed: int = 0) -> dict:
    key = jax.random.key(seed)
    ks = jax.random.split(key, 20)
    f32 = jnp.float32
    nrm = lambda k, s, sc: jax.random.normal(k, s, f32) * sc
    return {
        "x_prompt": nrm(ks[0], (BATCH, SEQ, D_MODEL), 1.0),
        "x_sample": nrm(ks[1], (DEC_BATCH, DEC_SEQ, D_MODEL), 1.0),
        "cache_k": nrm(ks[2], (DEPTH, DEC_BATCH, PAST_LEN, ATT_HEADS, HEAD_DIM), 1.0),
        "cache_v": nrm(ks[3], (DEPTH, DEC_BATCH, PAST_LEN, ATT_HEADS, HEAD_DIM), 1.0),
        "state_conv": nrm(ks[4], (DEPTH, DEC_BATCH, CONV_K - 1, CONV_CH), 0.5),
        "meta": nrm(ks[5], (N_META, D_MODEL), 1.0),
        "g_in": 1.0 + nrm(ks[6], (D_MODEL,), 0.02),
        "b_in": nrm(ks[7], (D_MODEL,), 0.02),
        "w_in": nrm(ks[8], (DEPTH, D_MODEL, IN_COLS), D_MODEL ** -0.5),
        "w_dw": nrm(ks[9], (DEPTH, CONV_K, CONV_CH), CONV_K ** -0.5),
        "b_dw": nrm(ks[10], (DEPTH, CONV_CH), 0.02),
        "g_conv": 1.0 + nrm(ks[11], (DEPTH, CONV_CH), 0.02),
        "b_conv": nrm(ks[12], (DEPTH, CONV_CH), 0.02),
        "w_out": nrm(ks[13], (DEPTH, D_MODEL, D_MODEL), D_MODEL ** -0.5 * BETA_INIT),
        "g_ln1": 1.0 + nrm(ks[14], (DEPTH, D_MODEL), 0.02),
        "b_ln1": nrm(ks[15], (DEPTH, D_MODEL), 0.02),
        "w_ff1": nrm(ks[16], (DEPTH, D_MODEL, D_FF), D_MODEL ** -0.5),
        "w_ff2": nrm(ks[17], (DEPTH, D_FF, D_MODEL), D_FF ** -0.5 * BETA_INIT),
        "g_ln2": 1.0 + nrm(ks[18], (DEPTH, D_MODEL), 0.02),
        "b_ln2": nrm(ks[19], (DEPTH, D_MODEL), 0.02),
    }


def reference(x_prompt, x_sample, cache_k, cache_v, state_conv, meta, g_in, b_in, w_in,
              w_dw, b_dw, g_conv, b_conv, w_out, g_ln1, b_ln1, w_ff1, w_ff2, g_ln2, b_ln2):
    B = x_prompt.shape[0]
    DB, n = x_sample.shape[0], x_sample.shape[1]
    P = cache_k.shape[2]
    meta_b = jnp.broadcast_to(meta[None].astype(x_prompt.dtype), (B, N_META, D_MODEL))
    hp = layer_norm(jnp.concatenate([meta_b, x_prompt], axis=1), g_in, b_in)
    hs = layer_norm(x_sample, g_in, b_in)
    q_pos_s = P + jnp.arange(n, dtype=jnp.int32)
    k_pos_s = jnp.arange(P + n, dtype=jnp.int32)
    kp_l, vp_l, cp_l, ks_l, vs_l, cs_l = [], [], [], [], [], []
    for l in range(DEPTH):
        qp, kp, vp, ap, gp = project_in(hp, w_in[l])
        att_p = stick_breaking_prompt(qp, kp, vp)
        buf0 = jnp.zeros((B, CONV_K - 1, CONV_CH), ap.dtype)
        conv_p, cst_p = conv_module(ap, gp, buf0, w_dw[l], b_dw[l], g_conv[l], b_conv[l])
        hp = layer_tail(hp, att_p, conv_p, w_out[l], g_ln1[l], b_ln1[l],
                        w_ff1[l], w_ff2[l], g_ln2[l], b_ln2[l])
        kp_l.append(kp); vp_l.append(vp); cp_l.append(cst_p)
        qs, kn, vn, an, gn = project_in(hs, w_in[l])
        k_all = jnp.concatenate([cache_k[l].astype(kn.dtype), kn], axis=1)
        v_all = jnp.concatenate([cache_v[l].astype(vn.dtype), vn], axis=1)
        att_s = stick_breaking_block(qs, k_all, v_all, q_pos_s, k_pos_s)
        conv_s, cst_s = conv_module(an, gn, state_conv[l].astype(an.dtype),
                                    w_dw[l], b_dw[l], g_conv[l], b_conv[l])
        hs = layer_tail(hs, att_s, conv_s, w_out[l], g_ln1[l], b_ln1[l],
                        w_ff1[l], w_ff2[l], g_ln2[l], b_ln2[l])
        ks_l.append(kn); vs_l.append(vn); cs_l.append(cst_s)
    y_prompt = hp[:, N_META:]
    return (y_prompt, hs, jnp.stack(kp_l), jnp.stack(vp_l), jnp.stack(cp_l),
            jnp.stack(ks_l), jnp.stack(vs_l), jnp.stack(cs_l))
```

```python
import functools

import jax
import jax.numpy as jnp
from jax import lax
from jax.experimental import pallas as pl
from jax.experimental.pallas import tpu as pltpu

D_MODEL = 1024
N_META = 16
ATT_HEADS = 8
HEAD_DIM = 64
ATT_WIDTH = ATT_HEADS * HEAD_DIM
CONV_CH = D_MODEL - ATT_WIDTH
CONV_K = 31
D_FF = 4 * D_MODEL
DEPTH = 1
ALPHA = (2.0 * DEPTH) ** 0.25
LN_EPS = 1e-5
Q_SCALE = HEAD_DIM ** -0.5

LANES = 128
HEADS_PER_GROUP = LANES // HEAD_DIM
N_GROUPS = ATT_WIDTH // LANES
HALO = 32
VMEM_LIMIT = 56 * 1024 * 1024

F32 = jnp.float32
BF16 = jnp.bfloat16


def _layer_norm(x, g, b):
    mu = jnp.mean(x, axis=-1, keepdims=True)
    xc = x - mu
    var = jnp.mean(xc * xc, axis=-1, keepdims=True)
    return xc * lax.rsqrt(var + LN_EPS) * g + b


def _in_proj_kernel(x_ref, g_ref, b_ref, w_ref, q_ref, k_ref, kb_ref, v_ref, vb_ref, u_ref):
    h = _layer_norm(x_ref[...], g_ref[...], b_ref[...]).astype(BF16)

    def proj(c):
        return jnp.dot(h, w_ref[:, c * ATT_WIDTH:(c + 1) * ATT_WIDTH], preferred_element_type=F32)

    q_ref[...] = (proj(0) * Q_SCALE).astype(BF16)
    k = proj(1)
    k_ref[...] = k
    kb_ref[...] = k.astype(BF16)
    v = proj(2)
    v_ref[...] = v
    vb_ref[...] = v.astype(BF16)
    u_ref[...] = proj(3) * jax.nn.sigmoid(proj(4))


def _in_proj(x, g, b, w_bf16, tm):
    m = x.shape[0]
    assert m % tm == 0
    row = lambda i: (i, 0)
    const = lambda i: (0, 0)
    out_f32 = jax.ShapeDtypeStruct((m, ATT_WIDTH), F32)
    out_bf16 = jax.ShapeDtypeStruct((m, ATT_WIDTH), BF16)
    blk = pl.BlockSpec((tm, ATT_WIDTH), row)
    return pl.pallas_call(
        _in_proj_kernel,
        name="in_proj",
        grid=(m // tm,),
        in_specs=[pl.BlockSpec((tm, D_MODEL), row),
                  pl.BlockSpec((1, D_MODEL), const),
                  pl.BlockSpec((1, D_MODEL), const),
                  pl.BlockSpec(w_bf16.shape, const)],
        out_specs=[blk] * 6,
        out_shape=[out_bf16, out_f32, out_bf16, out_f32, out_bf16, out_f32],
        compiler_params=pltpu.CompilerParams(
            dimension_semantics=("parallel",), vmem_limit_bytes=VMEM_LIMIT),
    )(x, g, b, w_bf16)


def _attn_kernel(q_ref, kf_ref, vf_ref, kp_ref, vp_ref, w_ref, o_ref, acc_ref, carry_ref,
                 *, nq, tq, tkd, tkp, p_valid):
    i = pl.program_id(2)
    lane = lax.broadcasted_iota(jnp.int32, (1, LANES), 1)
    head_lanes = [(lane >= h * HEAD_DIM) & (lane < (h + 1) * HEAD_DIM) for h in range(HEADS_PER_GROUP)]
    q = q_ref[0]
    qm = [jnp.where(head_lanes[h], q, jnp.zeros_like(q)) for h in range(HEADS_PER_GROUP)]
    w_scan = w_ref[...]

    acc_ref[...] = jnp.zeros_like(acc_ref)
    carry_ref[...] = jnp.zeros_like(carry_ref)

    def process_block(kblk, vblk, mask):
        tk = kblk.shape[0]
        pv = None
        for h in range(HEADS_PER_GROUP):
            z = lax.dot_general(qm[h], kblk, (((1,), (1,)), ((), ())), preferred_element_type=F32)
            parts = [None] * (tk // LANES)
            for s in reversed(range(tk // LANES)):
                zs = z[:, s * LANES:(s + 1) * LANES]
                softplus = jnp.log(1.0 + jnp.exp(-jnp.abs(zs)))
                log_beta = jnp.minimum(zs, 0.0) - softplus
                log_1m = log_beta - zs
                if mask is not None:
                    ms = mask[:, s * LANES:(s + 1) * LANES]
                    log_1m = jnp.where(ms, log_1m, 0.0)
                hi = log_1m.astype(BF16)
                lo = (log_1m - hi.astype(F32)).astype(BF16)
                scan = jnp.dot(jnp.concatenate([hi, lo], axis=1), w_scan, preferred_element_type=F32)
                carry = carry_ref[h]
                a = jnp.exp(log_beta + scan[:, :LANES] + carry)
                if mask is not None:
                    a = jnp.where(ms, a, 0.0)
                carry_ref[h] = carry + scan[:, LANES:]
                parts[s] = a.astype(BF16)
            a_full = parts[0] if len(parts) == 1 else jnp.concatenate(parts, axis=1)
            vh = jnp.where(head_lanes[h], vblk, jnp.zeros_like(vblk))
            d = jnp.dot(a_full, vh, preferred_element_type=F32)
            pv = d if pv is None else pv + d
        acc_ref[...] += pv

    row_id = lax.broadcasted_iota(jnp.int32, (tq, tkd), 0)
    col_id = lax.broadcasted_iota(jnp.int32, (tq, tkd), 1)
    d0 = pl.multiple_of(i * tq, tq)
    process_block(kf_ref[0, pl.ds(d0, tkd), :], vf_ref[0, pl.ds(d0, tkd), :], col_id < row_id)

    def frame_body(t, _):
        j0 = pl.multiple_of((i - 1 - t) * tq, tq)
        process_block(kf_ref[0, pl.ds(j0, tq), :], vf_ref[0, pl.ds(j0, tq), :], None)
        return 0
    if nq > 1:
        lax.fori_loop(0, i, frame_body, 0)

    n_full, rem = divmod(p_valid, tkp)
    if rem:
        col_p = lax.broadcasted_iota(jnp.int32, (tq, tkp), 1)
        process_block(kp_ref[0, pl.ds(n_full * tkp, tkp), :].astype(BF16),
                      vp_ref[0, pl.ds(n_full * tkp, tkp), :].astype(BF16), col_p < rem)

    def prefix_body(t, _):
        j0 = pl.multiple_of((n_full - 1 - t) * tkp, tkp)
        process_block(kp_ref[0, pl.ds(j0, tkp), :].astype(BF16),
                      vp_ref[0, pl.ds(j0, tkp), :].astype(BF16), None)
        return 0
    if n_full:
        lax.fori_loop(0, n_full, prefix_body, 0)

    o_ref[0] = acc_ref[...].astype(o_ref.dtype)


def _scan_weights():
    j = lax.broadcasted_iota(jnp.int32, (2 * LANES, 2 * LANES), 0) % LANES
    s = lax.broadcasted_iota(jnp.int32, (2 * LANES, 2 * LANES), 1)
    return ((s >= LANES) | (j > s)).astype(BF16)


def _attention(q, kf, vf, kp, vp, *, tq, tkd, tkp, p_valid, prefix_per_stream):
    nb, t, _ = q.shape
    lf = kf.shape[1]
    p_pad = kp.shape[1]
    assert t % tq == 0 and p_pad % tkp == 0 and p_valid <= p_pad
    assert tkd == tq or (t == tq and lf == tkd)
    nq = t // tq
    pmap = (lambda b, g, i: (b, 0, g)) if prefix_per_stream else (lambda b, g, i: (0, 0, g))
    kernel = functools.partial(_attn_kernel, nq=nq, tq=tq, tkd=tkd, tkp=tkp, p_valid=p_valid)
    return pl.pallas_call(
        kernel,
        name="attn",
        grid=(nb, N_GROUPS, nq),
        in_specs=[pl.BlockSpec((1, tq, LANES), lambda b, g, i: (b, i, g)),
                  pl.BlockSpec((1, lf, LANES), lambda b, g, i: (b, 0, g)),
                  pl.BlockSpec((1, lf, LANES), lambda b, g, i: (b, 0, g)),
                  pl.BlockSpec((1, p_pad, LANES), pmap),
                  pl.BlockSpec((1, p_pad, LANES), pmap),
                  pl.BlockSpec((2 * LANES, 2 * LANES), lambda b, g, i: (0, 0))],
        out_specs=pl.BlockSpec((1, tq, LANES), lambda b, g, i: (b, i, g)),
        out_shape=jax.ShapeDtypeStruct((nb, t, ATT_WIDTH), BF16),
        scratch_shapes=[pltpu.VMEM((tq, LANES), F32),
                        pltpu.VMEM((HEADS_PER_GROUP, tq, LANES), F32)],
        compiler_params=pltpu.CompilerParams(
            dimension_semantics=("parallel", "parallel", "arbitrary"), vmem_limit_bytes=VMEM_LIMIT),
    )(q, kf, vf, kp, vp, _scan_weights())


def _conv_kernel(u_ref, halo_ref, pre_ref, w_ref, bdw_ref, g_ref, b_ref, o_ref, xp_ref, *, tt, rows):
    i = pl.program_id(1)
    n_col = CONV_CH // LANES
    col = lambda c: slice(c * LANES, (c + 1) * LANES)

    @pl.when(i == 0)
    def _():
        for c in range(n_col):
            xp_ref[c, 0:HALO, :] = pre_ref[0, :, col(c)]

    @pl.when(i > 0)
    def _():
        for c in range(n_col):
            xp_ref[c, 0:HALO, :] = halo_ref[0, :, col(c)]

    for c in range(n_col):
        xp_ref[c, HALO:, :] = u_ref[0, :, col(c)]
    shift = HALO - (CONV_K - 1)

    for r0 in range(0, tt, rows):
        accs = []
        for c in range(n_col):
            acc = jnp.zeros((rows, LANES), F32)
            for k in range(CONV_K):
                acc = acc + w_ref[k:k + 1, col(c)] * xp_ref[c, pl.ds(r0 + shift + k, rows), :]
            accs.append(acc)
        y = jnp.concatenate(accs, axis=1) + bdw_ref[...]
        y = _layer_norm(y, g_ref[...], b_ref[...])
        o_ref[0, pl.ds(r0, rows), :] = (y * jax.nn.sigmoid(y)).astype(o_ref.dtype)


def _conv(u, prefix, w_dw, b_dw, g_c, b_c, *, tt, prefix_per_stream):
    nb, t, _ = u.shape
    assert t % tt == 0 and tt % HALO == 0
    rows = min(tt, 64)
    per = tt // HALO
    pmap = (lambda b, i: (b, 0, 0)) if prefix_per_stream else (lambda b, i: (0, 0, 0))
    const = lambda b, i: (0, 0)
    kernel = functools.partial(_conv_kernel, tt=tt, rows=rows)
    return pl.pallas_call(
        kernel,
        name="conv",
        grid=(nb, t // tt),
        in_specs=[pl.BlockSpec((1, tt, CONV_CH), lambda b, i: (b, i, 0)),
                  pl.BlockSpec((1, HALO, CONV_CH), lambda b, i: (b, jnp.maximum(i * per - 1, 0), 0)),
                  pl.BlockSpec((1, HALO, CONV_CH), pmap),
                  pl.BlockSpec((CONV_K, CONV_CH), const),
                  pl.BlockSpec((1, CONV_CH), const),
                  pl.BlockSpec((1, CONV_CH), const),
                  pl.BlockSpec((1, CONV_CH), const)],
        out_specs=pl.BlockSpec((1, tt, CONV_CH), lambda b, i: (b, i, 0)),
        out_shape=jax.ShapeDtypeStruct((nb, t, CONV_CH), BF16),
        scratch_shapes=[pltpu.VMEM((CONV_CH // LANES, HALO + tt, LANES), F32)],
        compiler_params=pltpu.CompilerParams(
            dimension_semantics=("parallel", "arbitrary"), vmem_limit_bytes=VMEM_LIMIT),
    )(u, u, prefix, w_dw, b_dw, g_c, b_c)


def _tail_kernel(x_ref, att_ref, conv_ref, gin_ref, bin_ref, wo_ref, g1_ref, b1_ref,
                 w1_ref, w2_ref, g2_ref, b2_ref, y_ref, *, ff_chunk):
    h = _layer_norm(x_ref[...], gin_ref[...], bin_ref[...])
    mix = jnp.dot(att_ref[...], wo_ref[:ATT_WIDTH, :], preferred_element_type=F32)
    mix = mix + jnp.dot(conv_ref[...], wo_ref[ATT_WIDTH:, :], preferred_element_type=F32)
    h1 = _layer_norm(ALPHA * h + mix, g1_ref[...], b1_ref[...])
    h1b = h1.astype(BF16)
    f = None
    for c in range(D_FF // ff_chunk):
        cols = slice(c * ff_chunk, (c + 1) * ff_chunk)
        r = jnp.maximum(jnp.dot(h1b, w1_ref[:, cols], preferred_element_type=F32), 0.0)
        d = jnp.dot((r * r).astype(BF16), w2_ref[cols, :], preferred_element_type=F32)
        f = d if f is None else f + d
    y_ref[...] = _layer_norm(ALPHA * h1 + f, g2_ref[...], b2_ref[...])


def _tail(x, att, conv, g_in, b_in, wo, g1, b1, w1, w2, g2, b2, tm):
    m = x.shape[0]
    assert m % tm == 0
    row = lambda i: (i, 0)
    const = lambda i: (0, 0)
    vec = pl.BlockSpec((1, D_MODEL), const)
    resident = lambda a: pl.BlockSpec(a.shape, const, pipeline_mode=pl.Buffered(1))
    kernel = functools.partial(_tail_kernel, ff_chunk=1024)
    return pl.pallas_call(
        kernel,
        name="tail",
        grid=(m // tm,),
        in_specs=[pl.BlockSpec((tm, D_MODEL), row),
                  pl.BlockSpec((tm, ATT_WIDTH), row),
                  pl.BlockSpec((tm, CONV_CH), row),
                  vec, vec, resident(wo), vec, vec, resident(w1), resident(w2), vec, vec],
        out_specs=pl.BlockSpec((tm, D_MODEL), row),
        out_shape=jax.ShapeDtypeStruct((m, D_MODEL), F32),
        compiler_params=pltpu.CompilerParams(
            dimension_semantics=("parallel",), vmem_limit_bytes=VMEM_LIMIT),
    )(x, att, conv, g_in, b_in, wo, g1, b1, w1, w2, g2, b2)


def kernel(x_prompt, x_sample, cache_k, cache_v, state_conv, meta, g_in, b_in, w_in,
           w_dw, b_dw, g_conv, b_conv, w_out, g_ln1, b_ln1, w_ff1, w_ff2, g_ln2, b_ln2):
    nb, seq, _ = x_prompt.shape
    db, n, _ = x_sample.shape
    past = cache_k.shape[2]
    assert DEPTH == 1 and seq >= CONV_K - 1 and n >= CONV_K - 1

    row = lambda a: a.reshape(1, -1)
    gin, bin_ = row(g_in), row(b_in)
    w_in_b = w_in[0].astype(BF16)
    wo_b, w1_b, w2_b = w_out[0].astype(BF16), w_ff1[0].astype(BF16), w_ff2[0].astype(BF16)
    conv_w = (w_dw[0], row(b_dw[0]), row(g_conv[0]), row(b_conv[0]))
    tail_w = (gin, bin_, wo_b, row(g_ln1[0]), row(b_ln1[0]), w1_b, w2_b, row(g_ln2[0]), row(b_ln2[0]))

    _, k_m, kb_m, v_m, vb_m, u_m = _in_proj(meta, gin, bin_, w_in_b, N_META)

    xp = x_prompt.reshape(nb * seq, D_MODEL)
    q_p, k_p, kb_p, v_p, vb_p, u_p = _in_proj(xp, gin, bin_, w_in_b, 512)
    to_seq = lambda a, b: a.reshape(b, -1, a.shape[-1])
    pad_meta = lambda a: jnp.pad(a, ((0, LANES - N_META), (0, 0)))[None]
    att_p = _attention(to_seq(q_p, nb), to_seq(kb_p, nb), to_seq(vb_p, nb), pad_meta(kb_m), pad_meta(vb_m),
                       tq=256, tkd=256, tkp=LANES, p_valid=N_META, prefix_per_stream=False)
    pre_p = jnp.concatenate([jnp.zeros((HALO - N_META, CONV_CH), F32), u_m], axis=0)[None]
    u_p = to_seq(u_p, nb)
    conv_p = _conv(u_p, pre_p, *conv_w, tt=512, prefix_per_stream=False)
    y_p = _tail(xp, att_p.reshape(nb * seq, ATT_WIDTH), conv_p.reshape(nb * seq, CONV_CH), *tail_w, 512)

    def with_meta(rows_m, rows_f):
        full = jnp.concatenate([jnp.broadcast_to(rows_m[None], (nb,) + rows_m.shape), to_seq(rows_f, nb)], axis=1)
        return full.reshape(1, nb, N_META + seq, ATT_HEADS, HEAD_DIM)
    cst_p = jnp.concatenate([pre_p[:, HALO - (CONV_K - 1):].repeat(nb, 0), u_p], axis=1)[:, -(CONV_K - 1):]

    xs = x_sample.reshape(db * n, D_MODEL)
    q_s, k_s, kb_s, v_s, vb_s, u_s = _in_proj(xs, gin, bin_, w_in_b, 512)
    pad_new = lambda a: jnp.pad(to_seq(a, db), ((0, 0), (0, LANES - n), (0, 0)))
    att_s = _attention(to_seq(q_s, db), pad_new(kb_s), pad_new(vb_s),
                       cache_k[0].reshape(db, past, ATT_WIDTH), cache_v[0].reshape(db, past, ATT_WIDTH),
                       tq=n, tkd=LANES, tkp=256, p_valid=past, prefix_per_stream=True)
    pre_s = jnp.concatenate([jnp.zeros((db, HALO - (CONV_K - 1), CONV_CH), F32), state_conv[0]], axis=1)
    u_s = to_seq(u_s, db)
    conv_s = _conv(u_s, pre_s, *conv_w, tt=n, prefix_per_stream=True)
    y_s = _tail(xs, att_s.reshape(db * n, ATT_WIDTH), conv_s.reshape(db * n, CONV_CH), *tail_w, 512)
    cst_s = jnp.concatenate([state_conv[0], u_s], axis=1)[:, -(CONV_K - 1):]

    heads = lambda a: a.reshape(1, db, n, ATT_HEADS, HEAD_DIM)
    return (y_p.reshape(nb, seq, D_MODEL), y_s.reshape(db, n, D_MODEL),
            with_meta(k_m, k_p), with_meta(v_m, v_p), cst_p[None],
            heads(k_s), heads(v_s), cst_s[None])
```

```python
import functools

import jax
import jax.numpy as jnp
from jax import lax
from jax.experimental import pallas as pl
from jax.experimental.pallas import tpu as pltpu

D_MODEL = 1024
N_META = 16
ATT_HEADS = 8
HEAD_DIM = 64
ATT_WIDTH = ATT_HEADS * HEAD_DIM
CONV_CH = D_MODEL - ATT_WIDTH
CONV_K = 31
D_FF = 4 * D_MODEL
DEPTH = 1
ALPHA = (2.0 * DEPTH) ** 0.25
LN_EPS = 1e-5
Q_SCALE = HEAD_DIM ** -0.5
LOG2E = 1.4426950408889634

LANES = 128
HEADS_PER_GROUP = LANES // HEAD_DIM
N_GROUPS = ATT_WIDTH // LANES
SCAN_W = 2 * LANES
TILE = 512
CTX = 16
VMEM_LIMIT = 56 * 1024 * 1024

F32 = jnp.float32
BF16 = jnp.bfloat16


def _layer_norm(x, g, b):
    mu = jnp.mean(x, axis=-1, keepdims=True)
    xc = x - mu
    var = jnp.mean(xc * xc, axis=-1, keepdims=True)
    return xc * lax.rsqrt(var + LN_EPS) * g + b


def _project(x, g, b, w_ref):
    h = _layer_norm(x, g, b).astype(BF16)

    def proj(c):
        return jnp.dot(h, w_ref[:, c * ATT_WIDTH:(c + 1) * ATT_WIDTH], preferred_element_type=F32)

    q = (proj(0) * Q_SCALE).astype(BF16)
    k = proj(1)
    v = proj(2)
    u = proj(3) * jax.nn.sigmoid(proj(4))
    return q, k, v, u


def _in_proj_tokens_kernel(x_ref, g_ref, b_ref, w_ref, q_ref, k_ref, kb_ref, v_ref, vb_ref, u_ref):
    q, k, v, u = _project(x_ref[...], g_ref[...], b_ref[...], w_ref)
    q_ref[...] = q
    k_ref[...] = k
    kb_ref[...] = k.astype(BF16)
    v_ref[...] = v
    vb_ref[...] = v.astype(BF16)
    u_ref[...] = u


def _in_proj_tokens(x, g, b, w_bf16, tm):
    m = x.shape[0]
    assert m % tm == 0
    row = lambda i: (i, 0)
    const = lambda i: (0, 0)
    out_f32 = jax.ShapeDtypeStruct((m, ATT_WIDTH), F32)
    out_bf16 = jax.ShapeDtypeStruct((m, ATT_WIDTH), BF16)
    blk = pl.BlockSpec((tm, ATT_WIDTH), row)
    return pl.pallas_call(
        _in_proj_tokens_kernel,
        name="in_proj_tokens",
        grid=(m // tm,),
        in_specs=[pl.BlockSpec((tm, D_MODEL), row),
                  pl.BlockSpec((1, D_MODEL), const),
                  pl.BlockSpec((1, D_MODEL), const),
                  pl.BlockSpec(w_bf16.shape, const)],
        out_specs=[blk] * 6,
        out_shape=[out_bf16, out_f32, out_bf16, out_f32, out_bf16, out_f32],
        compiler_params=pltpu.CompilerParams(
            dimension_semantics=("parallel",), vmem_limit_bytes=VMEM_LIMIT),
    )(x, g, b, w_bf16)


def _in_proj_prompt_kernel(x_ref, meta_ref, g_ref, b_ref, w_ref,
                           q_ref, kb_ref, vb_ref, u_ref, kt_ref, vt_ref, prev_ref, *, n_full):
    j = pl.program_id(1)

    @pl.when(j == 0)
    def _():
        prev_ref[...] = meta_ref[...]

    def run(rows):
        x = jnp.concatenate([prev_ref[...], x_ref[0, :rows - N_META, :]], axis=0)
        q, k, v, u = _project(x, g_ref[...], b_ref[...], w_ref)
        q_ref[0, :rows, :] = q
        kb_ref[0, :rows, :] = k.astype(BF16)
        vb_ref[0, :rows, :] = v.astype(BF16)
        u_ref[0, :rows, :] = u
        kt_ref[0, :, :rows] = k.T
        vt_ref[0, :, :rows] = v.T

    @pl.when(j < n_full)
    def _():
        run(TILE)
        prev_ref[...] = x_ref[0, TILE - N_META:, :]

    @pl.when(j == n_full)
    def _():
        run(LANES)


def _in_proj_prompt(x, meta, g, b, w_bf16):
    nb, seq, _ = x.shape
    assert seq % TILE == 0
    n_full = seq // TILE
    length = N_META + seq
    const = lambda bi, j: (0, 0)
    rows = lambda bi, j: (bi, j, 0)
    cols = lambda bi, j: (bi, 0, j)
    row_blk = pl.BlockSpec((1, TILE, ATT_WIDTH), rows)
    col_blk = pl.BlockSpec((1, ATT_WIDTH, TILE), cols)
    seq_bf16 = jax.ShapeDtypeStruct((nb, length, ATT_WIDTH), BF16)
    kernel = functools.partial(_in_proj_prompt_kernel, n_full=n_full)
    return pl.pallas_call(
        kernel,
        name="in_proj_prompt",
        grid=(nb, n_full + 1),
        in_specs=[pl.BlockSpec((1, TILE, D_MODEL), lambda bi, j: (bi, jnp.minimum(j, n_full - 1), 0)),
                  pl.BlockSpec((N_META, D_MODEL), const),
                  pl.BlockSpec((1, D_MODEL), const),
                  pl.BlockSpec((1, D_MODEL), const),
                  pl.BlockSpec(w_bf16.shape, const)],
        out_specs=[row_blk, row_blk, row_blk, row_blk, col_blk, col_blk],
        out_shape=[seq_bf16, seq_bf16, seq_bf16,
                   jax.ShapeDtypeStruct((nb, length, CONV_CH), F32),
                   jax.ShapeDtypeStruct((nb, ATT_WIDTH, length), F32),
                   jax.ShapeDtypeStruct((nb, ATT_WIDTH, length), F32)],
        scratch_shapes=[pltpu.VMEM((N_META, D_MODEL), F32)],
        compiler_params=pltpu.CompilerParams(
            dimension_semantics=("parallel", "arbitrary"), vmem_limit_bytes=VMEM_LIMIT),
    )(x, meta, g, b, w_bf16)


def _scan_weights():
    j = lax.broadcasted_iota(jnp.int32, (SCAN_W, SCAN_W), 0)
    s = lax.broadcasted_iota(jnp.int32, (SCAN_W, SCAN_W), 1)
    return (j > s).astype(BF16)


def _stick_block(z, carry, w_scan, mask):
    kw = z.shape[1]
    softplus = jnp.log(1.0 + jnp.exp2(jnp.abs(z) * -LOG2E))
    log_beta = jnp.minimum(z, 0.0) - softplus
    log_1m = log_beta - z
    if mask is not None:
        log_1m = jnp.where(mask, log_1m, 0.0)
    scan = jnp.dot(log_1m.astype(BF16), w_scan[:kw, :kw], preferred_element_type=F32)
    total = jnp.sum(log_1m, axis=1, keepdims=True)
    carry_w = carry if kw == LANES else jnp.concatenate([carry] * (kw // LANES), axis=1)
    a = jnp.exp2((log_beta + scan + carry_w) * LOG2E)
    if mask is not None:
        a = jnp.where(mask, a, 0.0)
    return a.astype(BF16), carry + total


def _attn_prompt_kernel(q_ref, k_ref, v_ref, w_ref, o_ref, acc_ref, carry_ref, *, tq):
    i = pl.program_id(2)
    lane = lax.broadcasted_iota(jnp.int32, (1, LANES), 1)
    q = q_ref[0, pl.ds(pl.multiple_of(N_META + i * tq, N_META), tq), :]
    zero = jnp.zeros_like(q)
    qh = [jnp.where((lane >= h * HEAD_DIM) & (lane < (h + 1) * HEAD_DIM), q, zero)
          for h in range(HEADS_PER_GROUP)]
    w_scan = w_ref[...]
    acc_ref[...] = jnp.zeros_like(acc_ref)
    carry_ref[...] = jnp.zeros_like(carry_ref)

    def process(k0, width, mask_fn):
        for h in range(HEADS_PER_GROUP):
            carry = carry_ref[h]
            pv = None
            for s in reversed(range(0, width, SCAN_W)):
                kw = min(SCAN_W, width - s)
                ks = pl.ds(k0 + s, kw)
                z = lax.dot_general(qh[h], k_ref[0, ks, :], (((1,), (1,)), ((), ())),
                                    preferred_element_type=F32)
                a, carry = _stick_block(z, carry, w_scan, None if mask_fn is None else mask_fn(s, kw))
                d = jnp.dot(a, v_ref[0, ks, :], preferred_element_type=F32)
                pv = d if pv is None else pv + d
            carry_ref[h] = carry
            acc_ref[h] += pv

    def causal(s, kw):
        row = lax.broadcasted_iota(jnp.int32, (tq, kw), 0)
        col = lax.broadcasted_iota(jnp.int32, (tq, kw), 1) + s
        return col < row

    def is_meta(s, kw):
        return lax.broadcasted_iota(jnp.int32, (tq, kw), 1) + s < N_META

    process(pl.multiple_of(N_META + i * tq, N_META), tq, causal)

    def earlier(t, _):
        process(pl.multiple_of(N_META + (i - 1 - t) * tq, N_META), tq, None)
        return 0
    lax.fori_loop(0, i, earlier, 0)

    process(0, LANES, is_meta)

    o_ref[0] = jnp.where(lane < HEAD_DIM, acc_ref[0], acc_ref[1]).astype(o_ref.dtype)


def _attn_prompt(q, k, v, *, tq):
    nb, length, _ = q.shape
    t = length - N_META
    assert t % tq == 0 and tq % SCAN_W == 0
    seq = pl.BlockSpec((1, length, LANES), lambda b, g, i: (b, 0, g))
    kernel = functools.partial(_attn_prompt_kernel, tq=tq)
    return pl.pallas_call(
        kernel,
        name="attn_prompt",
        grid=(nb, N_GROUPS, t // tq),
        in_specs=[seq, seq, seq, pl.BlockSpec((SCAN_W, SCAN_W), lambda b, g, i: (0, 0))],
        out_specs=pl.BlockSpec((1, tq, LANES), lambda b, g, i: (b, i, g)),
        out_shape=jax.ShapeDtypeStruct((nb, t, ATT_WIDTH), BF16),
        scratch_shapes=[pltpu.VMEM((HEADS_PER_GROUP, tq, LANES), F32),
                        pltpu.VMEM((HEADS_PER_GROUP, tq, LANES), F32)],
        compiler_params=pltpu.CompilerParams(
            dimension_semantics=("parallel", "parallel", "arbitrary"), vmem_limit_bytes=VMEM_LIMIT),
    )(q, k, v, _scan_weights())


def _attn_decode_kernel(q_ref, kn_ref, vn_ref, kt_ref, vt_ref, w_ref, o_ref, acc_ref, *, n, past):
    rows = ATT_HEADS * n
    q = q_ref[0]
    lane = lax.broadcasted_iota(jnp.int32, (1, ATT_WIDTH), 1)
    zero = jnp.zeros_like(q)
    in_head = [(lane >= h * HEAD_DIM) & (lane < (h + 1) * HEAD_DIM) for h in range(ATT_HEADS)]
    qx = jnp.concatenate([jnp.where(in_head[h], q, zero) for h in range(ATT_HEADS)], axis=0)
    w_scan = w_ref[...]

    t_id = lax.broadcasted_iota(jnp.int32, (ATT_HEADS, n, LANES), 1).reshape(rows, LANES)
    col = lax.broadcasted_iota(jnp.int32, (rows, LANES), 1)
    z = lax.dot_general(qx, kn_ref[0], (((1,), (1,)), ((), ())), preferred_element_type=F32)
    a, carry = _stick_block(z, jnp.zeros((rows, LANES), F32), w_scan, col < t_id)
    acc_ref[...] = jnp.dot(a, vn_ref[0], preferred_element_type=F32)

    for s in reversed(range(0, past, SCAN_W)):
        kt = kt_ref[0, :, s:s + SCAN_W].astype(BF16)
        vt = vt_ref[0, :, s:s + SCAN_W].astype(BF16)
        z = jnp.dot(qx, kt, preferred_element_type=F32)
        a, carry = _stick_block(z, carry, w_scan, None)
        acc_ref[...] += lax.dot_general(a, vt, (((1,), (1,)), ((), ())), preferred_element_type=F32)

    out = None
    for h in range(ATT_HEADS):
        part = jnp.where(in_head[h], acc_ref[h * n:(h + 1) * n, :], 0.0)
        out = part if out is None else out + part
    o_ref[0] = out.astype(o_ref.dtype)


def _attn_decode(q, k_new, v_new, kt_cache, vt_cache):
    db, n, _ = q.shape
    past = kt_cache.shape[2]
    assert past % SCAN_W == 0 and n <= LANES and k_new.shape[1] == LANES
    per = lambda shape: pl.BlockSpec((1,) + shape, lambda b: (b, 0, 0))
    kernel = functools.partial(_attn_decode_kernel, n=n, past=past)
    return pl.pallas_call(
        kernel,
        name="attn_decode",
        grid=(db,),
        in_specs=[per((n, ATT_WIDTH)), per((LANES, ATT_WIDTH)), per((LANES, ATT_WIDTH)),
                  per((ATT_WIDTH, past)), per((ATT_WIDTH, past)),
                  pl.BlockSpec((SCAN_W, SCAN_W), lambda b: (0, 0))],
        out_specs=per((n, ATT_WIDTH)),
        out_shape=jax.ShapeDtypeStruct((db, n, ATT_WIDTH), BF16),
        scratch_shapes=[pltpu.VMEM((ATT_HEADS * n, ATT_WIDTH), F32)],
        compiler_params=pltpu.CompilerParams(
            dimension_semantics=("parallel",), vmem_limit_bytes=VMEM_LIMIT),
    )(q, k_new, v_new, kt_cache, vt_cache, _scan_weights())


def _conv_kernel(u_ref, w_ref, bdw_ref, g_ref, b_ref, o_ref, xp_ref, *, tt, rows, lead):
    i = pl.program_id(1)
    n_col = CONV_CH // LANES
    col = lambda c: slice(c * LANES, (c + 1) * LANES)
    def fill_from(first_tile):
        start = pl.multiple_of(lead + tt * i - 2 * CTX, 8)
        for c in range(n_col):
            if first_tile:
                xp_ref[c, :2 * CTX - lead, :] = jnp.zeros((2 * CTX - lead, LANES), F32)
                xp_ref[c, 2 * CTX - lead:, :] = u_ref[0, :lead + tt, col(c)]
            else:
                xp_ref[c] = u_ref[0, pl.ds(start, 2 * CTX + tt), col(c)]

    if lead >= 2 * CTX:
        fill_from(False)
    else:
        pl.when(i == 0)(lambda: fill_from(True))
        pl.when(i > 0)(lambda: fill_from(False))

    shift = 2 * CTX - (CONV_K - 1)
    for r0 in range(0, tt, rows):
        accs = []
        for c in range(n_col):
            acc = jnp.zeros((rows, LANES), F32)
            for k in range(CONV_K):
                acc = acc + w_ref[k:k + 1, col(c)] * xp_ref[c, pl.ds(r0 + shift + k, rows), :]
            accs.append(acc)
        y = jnp.concatenate(accs, axis=1) + bdw_ref[...]
        y = _layer_norm(y, g_ref[...], b_ref[...])
        o_ref[0, pl.ds(r0, rows), :] = (y * jax.nn.sigmoid(y)).astype(o_ref.dtype)


def _conv(u, w_dw, b_dw, g_c, b_c, *, lead, tt):
    nb, total, _ = u.shape
    t = total - lead
    assert t % tt == 0 and lead % 8 == 0 and tt % 8 == 0 and CONV_K - 1 <= 2 * CTX
    rows = min(tt, 64)
    const = lambda b, i: (0, 0)
    kernel = functools.partial(_conv_kernel, tt=tt, rows=rows, lead=lead)
    return pl.pallas_call(
        kernel,
        name="conv",
        grid=(nb, t // tt),
        in_specs=[pl.BlockSpec((1, total, CONV_CH), lambda b, i: (b, 0, 0)),
                  pl.BlockSpec((CONV_K, CONV_CH), const),
                  pl.BlockSpec((1, CONV_CH), const),
                  pl.BlockSpec((1, CONV_CH), const),
                  pl.BlockSpec((1, CONV_CH), const)],
        out_specs=pl.BlockSpec((1, tt, CONV_CH), lambda b, i: (b, i, 0)),
        out_shape=jax.ShapeDtypeStruct((nb, t, CONV_CH), BF16),
        scratch_shapes=[pltpu.VMEM((CONV_CH // LANES, 2 * CTX + tt, LANES), F32)],
        compiler_params=pltpu.CompilerParams(
            dimension_semantics=("parallel", "arbitrary"), vmem_limit_bytes=VMEM_LIMIT),
    )(u, w_dw, b_dw, g_c, b_c)


def _tail_kernel(x_ref, att_ref, conv_ref, gin_ref, bin_ref, wo_ref, g1_ref, b1_ref,
                 w1_ref, w2_ref, g2_ref, b2_ref, y_ref, *, ff_chunk):
    h = _layer_norm(x_ref[...], gin_ref[...], bin_ref[...])
    mix = jnp.dot(att_ref[...], wo_ref[:ATT_WIDTH, :], preferred_element_type=F32)
    mix = mix + jnp.dot(conv_ref[...], wo_ref[ATT_WIDTH:, :], preferred_element_type=F32)
    h1 = _layer_norm(ALPHA * h + mix, g1_ref[...], b1_ref[...])
    h1b = h1.astype(BF16)
    f = None
    for c in range(D_FF // ff_chunk):
        cols = slice(c * ff_chunk, (c + 1) * ff_chunk)
        r = jnp.maximum(jnp.dot(h1b, w1_ref[:, cols], preferred_element_type=F32), 0.0)
        d = jnp.dot((r * r).astype(BF16), w2_ref[cols, :], preferred_element_type=F32)
        f = d if f is None else f + d
    y_ref[...] = _layer_norm(ALPHA * h1 + f, g2_ref[...], b2_ref[...])


def _tail(x, att, conv, g_in, b_in, wo, g1, b1, w1, w2, g2, b2, tm):
    m = x.shape[0]
    assert m % tm == 0
    row = lambda i: (i, 0)
    const = lambda i: (0, 0)
    vec = pl.BlockSpec((1, D_MODEL), const)
    resident = lambda a: pl.BlockSpec(a.shape, const, pipeline_mode=pl.Buffered(1))
    kernel = functools.partial(_tail_kernel, ff_chunk=1024)
    return pl.pallas_call(
        kernel,
        name="tail",
        grid=(m // tm,),
        in_specs=[pl.BlockSpec((tm, D_MODEL), row),
                  pl.BlockSpec((tm, ATT_WIDTH), row),
                  pl.BlockSpec((tm, CONV_CH), row),
                  vec, vec, resident(wo), vec, vec, resident(w1), resident(w2), vec, vec],
        out_specs=pl.BlockSpec((tm, D_MODEL), row),
        out_shape=jax.ShapeDtypeStruct((m, D_MODEL), F32),
        compiler_params=pltpu.CompilerParams(
            dimension_semantics=("parallel",), vmem_limit_bytes=VMEM_LIMIT),
    )(x, att, conv, g_in, b_in, wo, g1, b1, w1, w2, g2, b2)


def kernel(x_prompt, x_sample, cache_k, cache_v, state_conv, meta, g_in, b_in, w_in,
           w_dw, b_dw, g_conv, b_conv, w_out, g_ln1, b_ln1, w_ff1, w_ff2, g_ln2, b_ln2):
    nb, seq, _ = x_prompt.shape
    db, n, _ = x_sample.shape
    past = cache_k.shape[2]
    assert DEPTH == 1 and seq >= CONV_K - 1 and n >= CONV_K - 1

    row = lambda a: a.reshape(1, -1)
    gin, bin_ = row(g_in), row(b_in)
    w_in_b = w_in[0].astype(BF16)
    wo_b, w1_b, w2_b = w_out[0].astype(BF16), w_ff1[0].astype(BF16), w_ff2[0].astype(BF16)
    conv_w = (w_dw[0], row(b_dw[0]), row(g_conv[0]), row(b_conv[0]))
    tail_w = (gin, bin_, wo_b, row(g_ln1[0]), row(b_ln1[0]), w1_b, w2_b, row(g_ln2[0]), row(b_ln2[0]))
    state_rows = CONV_K - 1

    q_p, kb_p, vb_p, u_p, kt_p, vt_p = _in_proj_prompt(x_prompt, meta, gin, bin_, w_in_b)
    att_p = _attn_prompt(q_p, kb_p, vb_p, tq=TILE)
    conv_p = _conv(u_p, *conv_w, lead=N_META, tt=TILE)
    y_p = _tail(x_prompt.reshape(nb * seq, D_MODEL), att_p.reshape(nb * seq, ATT_WIDTH),
                conv_p.reshape(nb * seq, CONV_CH), *tail_w, TILE)
    to_cache = lambda a: a.reshape(nb, ATT_HEADS, HEAD_DIM, N_META + seq).transpose(0, 3, 1, 2)[None]
    cst_p = u_p[:, -state_rows:]

    xs = x_sample.reshape(db * n, D_MODEL)
    q_s, k_s, kb_s, v_s, vb_s, u_s = _in_proj_tokens(xs, gin, bin_, w_in_b, TILE)
    to_seq = lambda a: a.reshape(db, n, a.shape[-1])
    pad_new = lambda a: jnp.pad(to_seq(a), ((0, 0), (0, LANES - n), (0, 0)))
    from_cache = lambda c: c[0].transpose(0, 2, 3, 1).reshape(db, ATT_WIDTH, past)
    att_s = _attn_decode(to_seq(q_s), pad_new(kb_s), pad_new(vb_s), from_cache(cache_k), from_cache(cache_v))
    xu_s = jnp.concatenate([jnp.zeros((db, 2 * CTX - state_rows, CONV_CH), F32), state_conv[0], to_seq(u_s)], axis=1)
    conv_s = _conv(xu_s, *conv_w, lead=2 * CTX, tt=n)
    y_s = _tail(xs, att_s.reshape(db * n, ATT_WIDTH), conv_s.reshape(db * n, CONV_CH), *tail_w, TILE)
    cst_s = xu_s[:, -state_rows:]

    heads = lambda a: a.reshape(1, db, n, ATT_HEADS, HEAD_DIM)
    return (y_p.reshape(nb, seq, D_MODEL), y_s.reshape(db, n, D_MODEL),
            to_cache(kt_p), to_cache(vt_p), cst_p[None],
            heads(k_s), heads(v_s), cst_s[None])
```
